```python
import jax, jax.numpy as jnp
from jax import lax
import numpy as np

D_MODEL = 2048
BATCH = 2
SEQ = 16384
DEPTH = 4

M_HEADS = 4
M_DV = D_MODEL // M_HEADS
M_DK = M_DV // 2
M_QK = M_HEADS * M_DK
M_V = M_HEADS * M_DV
N_MGATES = 4 * M_HEADS
M_CHUNK = 128
LRU_WIDTH = D_MODEL
LRU_BLOCKS = 8
LRU_BS = LRU_WIDTH // LRU_BLOCKS
LRU_C = 8.0
CONV_W = 4
CONV_PAD = (2, 1)
N_EXPERTS = 32
N_GROUPS = 4
EXPERTS_PER_GROUP = N_EXPERTS // N_GROUPS
TOP_K = 2
D_EXPERT = 512
MOE_BLOCK = 256
EPS = 1e-6

OFF_K = M_QK
OFF_V = 2 * M_QK
OFF_O = OFF_V + M_V
OFF_MG = OFF_O + M_V
OFF_LRU = OFF_MG + N_MGATES
OFF_GA = OFF_LRU + LRU_WIDTH
OFF_GB = OFF_GA + D_MODEL
N_IN = OFF_GB + D_MODEL
IN_SPLITS = (OFF_K, OFF_V, OFF_O, OFF_MG, OFF_LRU, OFF_GA, OFF_GB)

kernel_name = 'hybrid_mlstm_rglru_grouped_moe_encoder'


def rmsnorm(x, g):
    xf = x.astype(jnp.float32)
    y = xf * lax.rsqrt(jnp.mean(xf * xf, axis=-1, keepdims=True) + EPS)
    return (y * g.astype(jnp.float32)).astype(x.dtype)


def dwconv_centred(x, w, b):
    y = lax.conv_general_dilated(x, w[:, None, :].astype(x.dtype), window_strides=(1,),
                                 padding=[CONV_PAD], dimension_numbers=('NWC', 'WIO', 'NWC'),
                                 feature_group_count=x.shape[-1])
    return y + b.astype(x.dtype)


def mlstm_scan(q, k, v, li, lf):
    B, H, S, dk = q.shape
    dv = v.shape[-1]
    L = M_CHUNK
    NC = S // L

    def chunks(t):
        return jnp.moveaxis(t.reshape((B, H, NC, L) + t.shape[3:]), 2, 0)

    tril = jnp.tril(jnp.ones((L, L), dtype=bool))

    def step(carry, xs):
        C, n, m = carry
        qc, kc, vc, ic, fc = xs
        b = jnp.cumsum(fc, axis=-1)
        Dm = jnp.where(tril, b[..., :, None] - b[..., None, :] + ic[..., None, :], -jnp.inf)
        m_inter = b + m[..., None]
        m_t = jnp.maximum(m_inter, jnp.max(Dm, axis=-1))
        Sm = jnp.exp(Dm - m_t[..., None]) * jnp.einsum('bhtk,bhsk->bhts', qc, kc)
        sc = jnp.exp(m_inter - m_t)
        num = jnp.einsum('bhts,bhsv->bhtv', Sm, vc) + sc[..., None] * jnp.einsum('bhtk,bhkv->bhtv', qc, C)
        den = jnp.sum(Sm, axis=-1) + sc * jnp.einsum('bhtk,bhk->bht', qc, n)
        h = num / jnp.maximum(jnp.abs(den), jnp.exp(-m_t))[..., None]
        g = b[..., -1]
        w_s = g[..., None] - b + ic
        m_new = jnp.maximum(g + m, jnp.max(w_s, axis=-1))
        decay = jnp.exp(g + m - m_new)
        ws = jnp.exp(w_s - m_new[..., None])
        kw = ws[..., None] * kc
        C = decay[..., None, None] * C + jnp.einsum('bhsk,bhsv->bhkv', kw, vc)
        n = decay[..., None] * n + jnp.sum(kw, axis=2)
        return (C, n, m_new), h

    init = (jnp.zeros((B, H, dk, dv), jnp.float32), jnp.zeros((B, H, dk), jnp.float32),
            jnp.zeros((B, H), jnp.float32))
    _, h = lax.scan(step, init, (chunks(q), chunks(k), chunks(v), chunks(li), chunks(lf)))
    return jnp.moveaxis(h, 0, 2).reshape(B, H, S, dv)


def mlstm_branch(q_raw, k_raw, v_raw, o_pre, mg_pre, conv_w, conv_b, norm_g):
    B, S, _ = q_raw.shape
    qk = jax.nn.silu(dwconv_centred(jnp.concatenate([q_raw, k_raw], axis=-1), conv_w, conv_b))
    q, k = jnp.split(qk.astype(jnp.float32), 2, axis=-1)

    def heads(t, d):
        return t.reshape(B, S, M_HEADS, d).transpose(0, 2, 1, 3)

    q = heads(q, M_DK) * (M_DK ** -0.5)
    k = heads(k, M_DK)
    v = heads(v_raw.astype(jnp.float32), M_DV)
    gates = jnp.transpose(mg_pre.astype(jnp.float32).reshape(B, S, 4, M_HEADS), (2, 0, 3, 1))
    i_f, f_f, i_b, f_b = gates[0], gates[1], gates[2], gates[3]
    h_f = mlstm_scan(q, k, v, i_f, jax.nn.log_sigmoid(f_f))
    rev = lambda t: jnp.flip(t, axis=2)
    h_b = rev(mlstm_scan(rev(q), rev(k), rev(v), rev(i_b), rev(jax.nn.log_sigmoid(f_b))))
    h = h_f + h_b
    h = h * lax.rsqrt(jnp.mean(h * h, axis=-1, keepdims=True) + EPS)
    h = h.transpose(0, 2, 1, 3).reshape(B, S, M_V) * norm_g.astype(jnp.float32)
    return (jax.nn.sigmoid(o_pre.astype(jnp.float32)) * h).astype(q_raw.dtype)


def _lin_comb(e1, e2):
    a1, b1 = e1
    a2, b2 = e2
    return a1 * a2, a2 * b1 + b2


def rglru_branch(x_raw, conv_w, conv_b, gate_w, gate_b, lam):
    B, S, C = x_raw.shape
    xc = dwconv_centred(x_raw, conv_w, conv_b).astype(jnp.float32)
    xb = xc.reshape(B, S, LRU_BLOCKS, LRU_BS)
    outs = []
    for d, reverse in ((0, False), (1, True)):
        g = jnp.einsum('bsnc,gncm->gbsnm', xb, gate_w[d].astype(jnp.float32)).reshape(2, B, S, C)
        g = g + gate_b[d].astype(jnp.float32)[:, None, None, :]
        r = jax.nn.sigmoid(g[0])
        i = jax.nn.sigmoid(g[1])
        log_a = -LRU_C * r * jax.nn.softplus(-lam[d].astype(jnp.float32))
        a = jnp.exp(log_a)
        bx = jnp.sqrt(-jnp.expm1(2.0 * log_a)) * (i * xc)
        _, h = lax.associative_scan(_lin_comb, (a, bx), axis=1, reverse=reverse)
        outs.append(h)
    return (outs[0] + outs[1]).astype(x_raw.dtype)


def route(xf, router_w, router_bias):
    T = xf.shape[0]
    scores = jax.nn.sigmoid(xf.astype(jnp.float32) @ router_w.astype(jnp.float32))
    biased = (scores + router_bias.astype(jnp.float32)).reshape(T, N_GROUPS, EXPERTS_PER_GROUP)
    grp_score = jnp.sum(lax.top_k(biased, 2)[0], axis=-1)
    sel_g = jnp.argmax(grp_score, axis=-1).astype(jnp.int32)
    in_grp = jnp.take_along_axis(biased, sel_g[:, None, None], axis=1)[:, 0]
    _, local = lax.top_k(in_grp, TOP_K)
    expert_idx = sel_g[:, None] * EXPERTS_PER_GROUP + local.astype(jnp.int32)
    w = jnp.take_along_axis(scores, expert_idx, axis=1)
    return expert_idx, w / jnp.sum(w, axis=-1, keepdims=True)


def moe_ffn(v, router_w, router_bias, wg, wu, wd):
    B, S, D = v.shape
    T = B * S
    xf = v.reshape(T, D)
    idx, w = route(xf, router_w, router_bias)
    A = T * TOP_K
    flat_e = idx.reshape(-1)
    flat_tok = jnp.repeat(jnp.arange(T, dtype=jnp.int32), TOP_K)
    flat_w = w.reshape(-1)
    order = jnp.argsort(flat_e)
    e_s = flat_e[order]
    counts = jnp.bincount(flat_e, length=N_EXPERTS).astype(jnp.int32)
    padded = (counts + MOE_BLOCK - 1) // MOE_BLOCK * MOE_BLOCK
    pad_end = jnp.cumsum(padded)
    pad_start = pad_end - padded
    start = jnp.cumsum(counts) - counts
    dest = pad_start[e_s] + jnp.arange(A, dtype=jnp.int32) - start[e_s]
    n_blocks = -(-(A + N_EXPERTS * (MOE_BLOCK - 1)) // MOE_BLOCK)
    P = n_blocks * MOE_BLOCK
    tok_buf = jnp.full((P,), T, jnp.int32).at[dest].set(flat_tok[order])
    w_buf = jnp.zeros((P,), jnp.float32).at[dest].set(flat_w[order])
    blk_e = jnp.minimum(jnp.searchsorted(pad_end, jnp.arange(n_blocks, dtype=jnp.int32) * MOE_BLOCK,
                                         side='right'), N_EXPERTS - 1)
    xpad = jnp.concatenate([xf, jnp.zeros((1, D), xf.dtype)], axis=0)

    def expert_block(args):
        tok, e = args
        xb = xpad[tok]
        h = jax.nn.silu(xb @ wg[e]) * (xb @ wu[e])
        return h @ wd[e]

    yb = lax.map(expert_block, (tok_buf.reshape(n_blocks, MOE_BLOCK), blk_e))
    y = jnp.zeros((T + 1, D), jnp.float32).at[tok_buf].add(yb.reshape(P, D).astype(jnp.float32) * w_buf[:, None])
    return y[:T].reshape(B, S, D).astype(v.dtype)


def setup_inputs(seed: int = 0) -> dict:
    key = jax.random.key(seed)
    ks = jax.random.split(key, 24)
    f32 = jnp.float32
    nrm = lambda k, shape, s: jax.random.normal(k, shape, f32) * s
    gain = lambda k, shape: 1.0 + 0.02 * jax.random.normal(k, shape, f32)
    x = jax.random.normal(ks[0], (BATCH, SEQ, D_MODEL), f32)
    f_offset = jnp.array([0.0, 3.0, 0.0, 3.0], f32)[None, :, None]
    b_mgates = (f_offset + nrm(ks[3], (DEPTH, 4, M_HEADS), 0.5)).reshape(DEPTH, N_MGATES)
    a0 = jax.random.uniform(ks[11], (DEPTH, 2, LRU_WIDTH), f32, minval=0.9, maxval=0.999)
    s = a0 ** (1.0 / LRU_C)
    lru_lambda = jnp.log(s) - jnp.log1p(-s)
    return {
        'x': x,
        'norm_mix_g': gain(ks[1], (DEPTH, D_MODEL)),
        'w_in': nrm(ks[2], (DEPTH, D_MODEL, N_IN), D_MODEL ** -0.5),
        'b_mgates': b_mgates,
        'conv_qk_w': nrm(ks[4], (DEPTH, CONV_W, 2 * M_QK), CONV_W ** -0.5),
        'conv_qk_b': nrm(ks[5], (DEPTH, 2 * M_QK), 0.01),
        'mlstm_norm_g': gain(ks[6], (DEPTH, M_V)),
        'conv_lru_w': nrm(ks[7], (DEPTH, CONV_W, LRU_WIDTH), CONV_W ** -0.5),
        'conv_lru_b': nrm(ks[8], (DEPTH, LRU_WIDTH), 0.01),
        'lru_gate_w': nrm(ks[9], (DEPTH, 2, 2, LRU_BLOCKS, LRU_BS, LRU_BS), LRU_BS ** -0.5),
        'lru_gate_b': nrm(ks[10], (DEPTH, 2, 2, LRU_WIDTH), 0.01),
        'lru_lambda': lru_lambda,
        'w_out': nrm(ks[12], (DEPTH, D_MODEL, D_MODEL), D_MODEL ** -0.5),
        'norm_ffn_g': gain(ks[13], (DEPTH, D_MODEL)),
        'router_w': nrm(ks[14], (D_MODEL, N_EXPERTS), D_MODEL ** -0.5),
        'router_bias': nrm(ks[15], (N_EXPERTS,), 0.01),
        'w_e_gate': nrm(ks[16], (DEPTH, N_EXPERTS, D_MODEL, D_EXPERT), D_MODEL ** -0.5),
        'w_e_up': nrm(ks[17], (DEPTH, N_EXPERTS, D_MODEL, D_EXPERT), D_MODEL ** -0.5),
        'w_e_down': nrm(ks[18], (DEPTH, N_EXPERTS, D_EXPERT, D_MODEL), D_EXPERT ** -0.5),
        'norm_final_g': gain(ks[19], (D_MODEL,)),
    }


def reference(x, norm_mix_g, w_in, b_mgates, conv_qk_w, conv_qk_b, mlstm_norm_g, conv_lru_w, conv_lru_b,
              lru_gate_w, lru_gate_b, lru_lambda, w_out, norm_ffn_g, router_w, router_bias,
              w_e_gate, w_e_up, w_e_down, norm_final_g):
    for l in range(DEPTH):
        u = rmsnorm(x, norm_mix_g[l])
        z = u @ w_in[l]
        q_raw, k_raw, v_m, o_pre, mg_pre, x_lru, ga_pre, gb_pre = jnp.split(z, IN_SPLITS, axis=-1)
        y_a = mlstm_branch(q_raw, k_raw, v_m, o_pre, mg_pre + b_mgates[l],
                           conv_qk_w[l], conv_qk_b[l], mlstm_norm_g[l])
        y_b = rglru_branch(x_lru, conv_lru_w[l], conv_lru_b[l], lru_gate_w[l], lru_gate_b[l], lru_lambda[l])
        mixed = jax.nn.sigmoid(ga_pre) * y_a + jax.nn.sigmoid(gb_pre) * y_b
        x = x + mixed @ w_out[l]
        x = x + moe_ffn(rmsnorm(x, norm_ffn_g[l]), router_w, router_bias, w_e_gate[l], w_e_up[l], w_e_down[l])
    return rmsnorm(x, norm_final_g)
```

```python
import functools

import jax
import jax.numpy as jnp
from jax import lax
from jax.experimental import pallas as pl
from jax.experimental.pallas import tpu as pltpu

M_HEADS = 4
LRU_BLOCKS = 8
LRU_C = 8.0
CONV_W = 4
N_EXPERTS = 32
N_GROUPS = 4
EXPERTS_PER_GROUP = N_EXPERTS // N_GROUPS
TOP_K = 2
MOE_BLOCK = 256
EPS = 1e-6

LANES = 128
SUBLANES = 8
VMEM_LIMIT_BYTES = 56 * 1024 * 1024

MLSTM_CHUNK = 256


def _pick_tile(n, pref):
    t = min(n, pref)
    while n % t:
        t //= 2
    return t


def _softplus(x):
    return jnp.maximum(x, 0.0) + jnp.log1p(jnp.exp(-jnp.abs(x)))


def _cparams(sem):
    return pltpu.CompilerParams(dimension_semantics=sem, vmem_limit_bytes=VMEM_LIMIT_BYTES)


def _inproj_body(x_ref, g_ref, w_ref, wg_ref, bg_ref, z_ref, gc_ref, u_ref):
    j = pl.program_id(1)

    @pl.when(j == 0)
    def _():
        xf = x_ref[...]
        u = xf * lax.rsqrt(jnp.mean(xf * xf, axis=-1, keepdims=True) + EPS) * g_ref[...]
        u_ref[...] = u.astype(jnp.bfloat16)
        gc_ref[...] = jnp.dot(u, wg_ref[...], preferred_element_type=jnp.float32,
                              precision=lax.Precision.HIGHEST) + bg_ref[...]

    z_ref[...] = jnp.dot(u_ref[...], w_ref[...], preferred_element_type=jnp.float32).astype(z_ref.dtype)


def _inproj(x2, g, w_main, w_gates, b_gates):
    T, D = x2.shape
    N = w_main.shape[1]
    tm = _pick_tile(T, 1024)
    tn = _pick_tile(N, 1024)
    return pl.pallas_call(
        _inproj_body,
        grid=(T // tm, N // tn),
        in_specs=[
            pl.BlockSpec((tm, D), lambda i, j: (i, 0)),
            pl.BlockSpec((1, D), lambda i, j: (0, 0)),
            pl.BlockSpec((D, tn), lambda i, j: (0, j)),
            pl.BlockSpec((D, LANES), lambda i, j: (0, 0)),
            pl.BlockSpec((1, LANES), lambda i, j: (0, 0)),
        ],
        out_specs=[
            pl.BlockSpec((tm, tn), lambda i, j: (i, j)),
            pl.BlockSpec((tm, LANES), lambda i, j: (i, 0)),
        ],
        out_shape=[
            jax.ShapeDtypeStruct((T, N), jnp.bfloat16),
            jax.ShapeDtypeStruct((T, LANES), jnp.float32),
        ],
        scratch_shapes=[pltpu.VMEM((tm, D), jnp.bfloat16)],
        compiler_params=_cparams(("parallel", "arbitrary")),
        name="inproj",
    )(x2, g, w_main, w_gates, b_gates)


def _dwconv_body(x_ref, p_ref, n_ref, w_ref, b_ref, s_ref, o_ref, *, silu):
    i = pl.program_id(1)
    last = pl.num_programs(1) - 1
    cur = x_ref[0].astype(jnp.float32)
    ts = cur.shape[0]
    prev = jnp.where(i > 0, p_ref[0].astype(jnp.float32), 0.0)
    nxt = jnp.where(i < last, n_ref[0].astype(jnp.float32), 0.0)
    row = lax.broadcasted_iota(jnp.int32, cur.shape, 0)
    x_m1 = jnp.where(row == 0, prev[7:8], pltpu.roll(cur, 1, 0))
    x_m2 = pltpu.roll(cur, 2, 0)
    x_m2 = jnp.where(row == 0, prev[6:7], jnp.where(row == 1, prev[7:8], x_m2))
    x_p1 = jnp.where(row == ts - 1, nxt[0:1], pltpu.roll(cur, ts - 1, 0))
    w = w_ref[...]
    y = x_m2 * w[0:1] + x_m1 * w[1:2] + cur * w[2:3] + x_p1 * w[3:4] + b_ref[...]
    if silu:
        y = y * jax.nn.sigmoid(y)
    o_ref[0] = (y * s_ref[...]).astype(o_ref.dtype)


def _dwconv(z3, col0, width, w, b, scale, *, silu, out_dtype):
    B, S, _ = z3.shape
    ts = _pick_tile(S, 512)
    cb = _pick_tile(width, 512)
    cblk0 = col0 // cb
    nh = ts // SUBLANES
    n_halo = S // SUBLANES
    return pl.pallas_call(
        functools.partial(_dwconv_body, silu=silu),
        grid=(B, S // ts, width // cb),
        in_specs=[
            pl.BlockSpec((1, ts, cb), lambda b_, i, j: (b_, i, cblk0 + j)),
            pl.BlockSpec((1, SUBLANES, cb), lambda b_, i, j: (b_, jnp.maximum(i * nh - 1, 0), cblk0 + j)),
            pl.BlockSpec((1, SUBLANES, cb), lambda b_, i, j: (b_, jnp.minimum((i + 1) * nh, n_halo - 1), cblk0 + j)),
            pl.BlockSpec((CONV_W, cb), lambda b_, i, j: (0, j)),
            pl.BlockSpec((1, cb), lambda b_, i, j: (0, j)),
            pl.BlockSpec((1, cb), lambda b_, i, j: (0, j)),
        ],
        out_specs=pl.BlockSpec((1, ts, cb), lambda b_, i, j: (b_, i, j)),
        out_shape=jax.ShapeDtypeStruct((B, S, width), out_dtype),
        compiler_params=_cparams(("parallel", "parallel", "parallel")),
        name="dwconv",
    )(z3, z3, z3, w, b, scale)


def _mlstm_body(q_ref, k_ref, v_ref, g_ref, o_ref, c_ref, n_ref, m_ref, *, reverse, gate_col0):
    h_idx = pl.program_id(1)

    @pl.when(pl.program_id(2) == 0)
    def _():
        c_ref[...] = jnp.zeros_like(c_ref)
        n_ref[...] = jnp.zeros_like(n_ref)
        m_ref[...] = jnp.zeros_like(m_ref)

    q = q_ref[...]
    k = k_ref[...]
    v = v_ref[...]
    L = q.shape[0]
    g = g_ref[...]
    gt = g.T
    lane = lax.broadcasted_iota(jnp.int32, g.shape, 1)
    sub = lax.broadcasted_iota(jnp.int32, gt.shape, 0)
    ci = gate_col0 + h_idx
    cf = gate_col0 + M_HEADS + h_idx
    i_col = jnp.sum(jnp.where(lane == ci, g, 0.0), axis=1, keepdims=True)
    f_col = jnp.sum(jnp.where(lane == cf, g, 0.0), axis=1, keepdims=True)
    i_row = jnp.sum(jnp.where(sub == ci, gt, 0.0), axis=0, keepdims=True)
    f_row = jnp.sum(jnp.where(sub == cf, gt, 0.0), axis=0, keepdims=True)
    lf_col = -_softplus(-f_col)
    lf_row = -_softplus(-f_row)

    t_idx = lax.broadcasted_iota(jnp.int32, (L, L), 0)
    s_idx = lax.broadcasted_iota(jnp.int32, (L, L), 1)
    causal = (s_idx >= t_idx) if reverse else (s_idx <= t_idx)
    b_col = jnp.sum(jnp.where(causal, lf_row, 0.0), axis=1, keepdims=True)
    anti = (t_idx >= s_idx) if reverse else (t_idx <= s_idx)
    b_row = jnp.sum(jnp.where(anti, lf_col, 0.0), axis=0, keepdims=True)
    g_tot = jnp.sum(lf_row, axis=1, keepdims=True)

    m_prev = m_ref[0:1, 0:1]
    dm = jnp.where(causal, b_col - b_row + i_row, -jnp.inf)
    m_inter = b_col + m_prev
    m_t = jnp.maximum(m_inter, jnp.max(dm, axis=1, keepdims=True))
    qk = lax.dot_general(q, k, (((1,), (1,)), ((), ())), preferred_element_type=jnp.float32)
    sm = jnp.exp(dm - m_t) * qk
    sc = jnp.exp(m_inter - m_t)
    c_prev = c_ref[...]
    n_prev = n_ref[...]
    num = jnp.dot(sm.astype(jnp.bfloat16), v, preferred_element_type=jnp.float32) + sc * jnp.dot(
        q, c_prev.astype(jnp.bfloat16), preferred_element_type=jnp.float32)
    den = jnp.sum(sm, axis=1, keepdims=True) + sc * jnp.sum(q.astype(jnp.float32) * n_prev, axis=1, keepdims=True)
    o_ref[...] = (num / jnp.maximum(jnp.abs(den), jnp.exp(-m_t))).astype(o_ref.dtype)

    w_s = g_tot - b_col + i_col
    m_new = jnp.maximum(g_tot + m_prev, jnp.max(w_s, axis=0, keepdims=True))
    decay = jnp.exp(g_tot + m_prev - m_new)
    kw = jnp.exp(w_s - m_new) * k.astype(jnp.float32)
    c_ref[...] = decay * c_prev + lax.dot_general(kw.astype(jnp.bfloat16), v, (((0,), (0,)), ((), ())),
                                                  preferred_element_type=jnp.float32)
    n_ref[...] = decay * n_prev + jnp.sum(kw, axis=0, keepdims=True)
    m_ref[...] = jnp.broadcast_to(m_new, m_ref.shape)


def _mlstm_scan(qk, z, gates, B, S, dk, dv, v_col0, *, reverse):
    T = B * S
    L = _pick_tile(S, MLSTM_CHUNK)
    nc = S // L
    vblk0 = v_col0 // dv
    gate_col0 = 2 * M_HEADS if reverse else 0

    def row(b_, c):
        return b_ * nc + (nc - 1 - c if reverse else c)

    return pl.pallas_call(
        functools.partial(_mlstm_body, reverse=reverse, gate_col0=gate_col0),
        grid=(B, M_HEADS, nc),
        in_specs=[
            pl.BlockSpec((L, dk), lambda b_, h, c: (row(b_, c), h)),
            pl.BlockSpec((L, dk), lambda b_, h, c: (row(b_, c), M_HEADS + h)),
            pl.BlockSpec((L, dv), lambda b_, h, c: (row(b_, c), vblk0 + h)),
            pl.BlockSpec((L, LANES), lambda b_, h, c: (row(b_, c), 0)),
        ],
        out_specs=pl.BlockSpec((L, dv), lambda b_, h, c: (row(b_, c), h)),
        out_shape=jax.ShapeDtypeStruct((T, M_HEADS * dv), jnp.float32),
        scratch_shapes=[pltpu.VMEM((dk, dv), jnp.float32), pltpu.VMEM((1, dk), jnp.float32),
                        pltpu.VMEM((SUBLANES, LANES), jnp.float32)],
        compiler_params=_cparams(("parallel", "parallel", "arbitrary")),
        name="mlstm_bwd" if reverse else "mlstm_fwd",
    )(qk, qk, z, gates)


def _lru_body(x_ref, w_ref, b_ref, lam_ref, o_ref, a_ref, bx_ref, h_ref, *, reverse):
    @pl.when(pl.program_id(2) == 0)
    def _():
        h_ref[...] = jnp.zeros_like(h_ref)

    xc = x_ref[...]
    ts = xc.shape[0]
    xb = xc.astype(jnp.bfloat16)
    gr = jnp.dot(xb, w_ref[0, 0, 0], preferred_element_type=jnp.float32) + b_ref[0, 0]
    gi = jnp.dot(xb, w_ref[0, 1, 0], preferred_element_type=jnp.float32) + b_ref[0, 1]
    log_a = (-LRU_C) * jax.nn.sigmoid(gr) * _softplus(-lam_ref[0])
    a = jnp.exp(log_a)
    a_ref[...] = a
    bx_ref[...] = jnp.sqrt(1.0 - a * a) * (jax.nn.sigmoid(gi) * xc)

    ngroups = ts // SUBLANES
    sub = lax.broadcasted_iota(jnp.int32, (SUBLANES, xc.shape[1]), 0)

    def group(gidx, h):
        gi_ = (ngroups - 1 - gidx) if reverse else gidx
        rows = pl.ds(pl.multiple_of(gi_ * SUBLANES, SUBLANES), SUBLANES)
        A = a_ref[rows, :]
        Bv = bx_ref[rows, :]
        for k in (1, 2, 4):
            if reverse:
                a_s = pltpu.roll(A, SUBLANES - k, 0)
                b_s = pltpu.roll(Bv, SUBLANES - k, 0)
                valid = sub < SUBLANES - k
            else:
                a_s = pltpu.roll(A, k, 0)
                b_s = pltpu.roll(Bv, k, 0)
                valid = sub >= k
            Bv = jnp.where(valid, A * b_s + Bv, Bv)
            A = jnp.where(valid, A * a_s, A)
        H = A * h + Bv
        o_ref[rows, :] = H
        return H[0:1, :] if reverse else H[SUBLANES - 1:SUBLANES, :]

    h_ref[...] = lax.fori_loop(0, ngroups, group, h_ref[...], unroll=4)


def _lru_scan(xc2, gate_w, gate_b, lam, B, S, *, reverse):
    T, C = xc2.shape
    bs = C // LRU_BLOCKS
    ts = _pick_tile(S, 512)
    nt = S // ts
    d = 1 if reverse else 0

    def row(b_, c):
        return b_ * nt + (nt - 1 - c if reverse else c)

    return pl.pallas_call(
        functools.partial(_lru_body, reverse=reverse),
        grid=(B, LRU_BLOCKS, nt),
        in_specs=[
            pl.BlockSpec((ts, bs), lambda b_, n, c: (row(b_, c), n)),
            pl.BlockSpec((1, 2, 1, bs, bs), lambda b_, n, c: (d, 0, n, 0, 0)),
            pl.BlockSpec((1, 2, 1, bs), lambda b_, n, c: (d, 0, 0, n)),
            pl.BlockSpec((1, 1, bs), lambda b_, n, c: (d, 0, n)),
        ],
        out_specs=pl.BlockSpec((ts, bs), lambda b_, n, c: (row(b_, c), n)),
        out_shape=jax.ShapeDtypeStruct((T, C), jnp.float32),
        scratch_shapes=[pltpu.VMEM((ts, bs), jnp.float32), pltpu.VMEM((ts, bs), jnp.float32),
                        pltpu.VMEM((1, bs), jnp.float32)],
        compiler_params=_cparams(("parallel", "parallel", "arbitrary")),
        name="lru_bwd" if reverse else "lru_fwd",
    )(xc2, gate_w, gate_b, lam)


def _outproj_body(x_ref, hf_ref, hb_ref, lf_ref, lb_ref, o_ref_, ga_ref, gb_ref, ng_ref, w_ref, out_ref):
    D = x_ref.shape[1]
    dv = D // M_HEADS
    h = hf_ref[...] + hb_ref[...]
    parts = []
    for hh in range(M_HEADS):
        hs = h[:, hh * dv:(hh + 1) * dv]
        parts.append(hs * lax.rsqrt(jnp.mean(hs * hs, axis=-1, keepdims=True) + EPS))
    hn = jnp.concatenate(parts, axis=1) * ng_ref[...]
    y_a = jax.nn.sigmoid(o_ref_[...].astype(jnp.float32)) * hn
    y_b = lf_ref[...] + lb_ref[...]
    mixed = (jax.nn.sigmoid(ga_ref[...].astype(jnp.float32)) * y_a
             + jax.nn.sigmoid(gb_ref[...].astype(jnp.float32)) * y_b)
    out_ref[...] = x_ref[...] + jnp.dot(mixed.astype(jnp.bfloat16), w_ref[...],
                                        preferred_element_type=jnp.float32)


def _outproj(x2, h_f, h_b, l_f, l_b, z, o_blk, ga_blk, gb_blk, norm_g, w_out):
    T, D = x2.shape
    tm = _pick_tile(T, 256)
    row = lambda i: (i, 0)
    return pl.pallas_call(
        _outproj_body,
        grid=(T // tm,),
        in_specs=[
            pl.BlockSpec((tm, D), row), pl.BlockSpec((tm, D), row), pl.BlockSpec((tm, D), row),
            pl.BlockSpec((tm, D), row), pl.BlockSpec((tm, D), row),
            pl.BlockSpec((tm, D), lambda i: (i, o_blk)),
            pl.BlockSpec((tm, D), lambda i: (i, ga_blk)),
            pl.BlockSpec((tm, D), lambda i: (i, gb_blk)),
            pl.BlockSpec((1, D), lambda i: (0, 0)),
            pl.BlockSpec((D, D), lambda i: (0, 0)),
        ],
        out_specs=pl.BlockSpec((tm, D), row),
        out_shape=jax.ShapeDtypeStruct((T, D), jnp.float32),
        compiler_params=_cparams(("parallel",)),
        name="outproj",
    )(x2, h_f, h_b, l_f, l_b, z, z, z, norm_g, w_out)


def _rmsnorm(x, g):
    xf = x.astype(jnp.float32)
    y = xf * lax.rsqrt(jnp.mean(xf * xf, axis=-1, keepdims=True) + EPS)
    return (y * g.astype(jnp.float32)).astype(x.dtype)


def _route(xf, router_w, router_bias):
    T = xf.shape[0]
    scores = jax.nn.sigmoid(jnp.dot(xf.astype(jnp.float32), router_w.astype(jnp.float32),
                                    precision=lax.Precision.HIGHEST))
    biased = (scores + router_bias.astype(jnp.float32)).reshape(T, N_GROUPS, EXPERTS_PER_GROUP)
    grp_score = jnp.sum(lax.top_k(biased, 2)[0], axis=-1)
    sel_g = jnp.argmax(grp_score, axis=-1).astype(jnp.int32)
    in_grp = jnp.take_along_axis(biased, sel_g[:, None, None], axis=1)[:, 0]
    _, local = lax.top_k(in_grp, TOP_K)
    expert_idx = sel_g[:, None] * EXPERTS_PER_GROUP + local.astype(jnp.int32)
    w = jnp.take_along_axis(scores, expert_idx, axis=1)
    return expert_idx, w / jnp.sum(w, axis=-1, keepdims=True)


def _moe_ffn(xf, router_w, router_bias, wg, wu, wd):
    T, D = xf.shape
    idx, w = _route(xf, router_w, router_bias)
    A = T * TOP_K
    flat_e = idx.reshape(-1)
    flat_tok = jnp.repeat(jnp.arange(T, dtype=jnp.int32), TOP_K)
    flat_w = w.reshape(-1)
    order = jnp.argsort(flat_e)
    e_s = flat_e[order]
    counts = jnp.bincount(flat_e, length=N_EXPERTS).astype(jnp.int32)
    padded = (counts + MOE_BLOCK - 1) // MOE_BLOCK * MOE_BLOCK
    pad_end = jnp.cumsum(padded)
    pad_start = pad_end - padded
    start = jnp.cumsum(counts) - counts
    dest = pad_start[e_s] + jnp.arange(A, dtype=jnp.int32) - start[e_s]
    n_blocks = -(-(A + N_EXPERTS * (MOE_BLOCK - 1)) // MOE_BLOCK)
    P = n_blocks * MOE_BLOCK
    tok_buf = jnp.full((P,), T, jnp.int32).at[dest].set(flat_tok[order])
    w_buf = jnp.zeros((P,), jnp.float32).at[dest].set(flat_w[order])
    blk_e = jnp.minimum(jnp.searchsorted(pad_end, jnp.arange(n_blocks, dtype=jnp.int32) * MOE_BLOCK,
                                         side='right'), N_EXPERTS - 1)
    xpad = jnp.concatenate([xf, jnp.zeros((1, D), xf.dtype)], axis=0)

    def expert_block(args):
        tok, e = args
        xb = xpad[tok]
        h = jax.nn.silu(xb @ wg[e]) * (xb @ wu[e])
        return h @ wd[e]

    yb = lax.map(expert_block, (tok_buf.reshape(n_blocks, MOE_BLOCK), blk_e))
    y = jnp.zeros((T + 1, D), jnp.float32).at[tok_buf].add(yb.reshape(P, D).astype(jnp.float32) * w_buf[:, None])
    return y[:T]


def kernel(x, norm_mix_g, w_in, b_mgates, conv_qk_w, conv_qk_b, mlstm_norm_g, conv_lru_w, conv_lru_b,
           lru_gate_w, lru_gate_b, lru_lambda, w_out, norm_ffn_g, router_w, router_bias,
           w_e_gate, w_e_up, w_e_down, norm_final_g):
    B, S, D = x.shape
    T = B * S
    depth = w_in.shape[0]
    m_v = mlstm_norm_g.shape[1]
    m_qk = conv_qk_w.shape[2] // 2
    n_mg = b_mgates.shape[1]
    lru_w = conv_lru_w.shape[2]
    dk = m_qk // M_HEADS
    dv = m_v // M_HEADS
    off_mg = 2 * m_qk + 2 * m_v
    off_lru = off_mg + n_mg
    assert m_v == D and lru_w == D and 2 * m_qk == D and n_mg == 4 * M_HEADS
    f32 = jnp.float32

    qk_scale = jnp.concatenate([jnp.full((1, m_qk), dk ** -0.5, f32), jnp.ones((1, m_qk), f32)], axis=1)
    ones_row = jnp.ones((1, D), f32)

    x2 = x.reshape(T, D)
    for l in range(depth):
        wl = w_in[l]
        w_main = jnp.concatenate([wl[:, :off_mg], wl[:, off_lru:]], axis=1).astype(jnp.bfloat16)
        w_gates = jnp.pad(wl[:, off_mg:off_lru], ((0, 0), (0, LANES - n_mg)))
        b_gates = jnp.pad(b_mgates[l][None, :], ((0, 0), (0, LANES - n_mg)))
        z, gates = _inproj(x2, norm_mix_g[l][None, :], w_main, w_gates, b_gates)
        z3 = z.reshape(B, S, -1)

        qk = _dwconv(z3, 0, D, conv_qk_w[l], conv_qk_b[l][None, :], qk_scale,
                     silu=True, out_dtype=jnp.bfloat16).reshape(T, D)
        h_f = _mlstm_scan(qk, z, gates, B, S, dk, dv, D, reverse=False)
        h_b = _mlstm_scan(qk, z, gates, B, S, dk, dv, D, reverse=True)

        xc = _dwconv(z3, 3 * D, D, conv_lru_w[l], conv_lru_b[l][None, :], ones_row,
                     silu=False, out_dtype=f32).reshape(T, D)
        gw = lru_gate_w[l].astype(jnp.bfloat16)
        gb = lru_gate_b[l][:, :, None, :]
        lam = lru_lambda[l][:, None, :]
        l_f = _lru_scan(xc, gw, gb, lam, B, S, reverse=False)
        l_b = _lru_scan(xc, gw, gb, lam, B, S, reverse=True)

        x2 = _outproj(x2, h_f, h_b, l_f, l_b, z, 2, 4, 5, mlstm_norm_g[l][None, :],
                      w_out[l].astype(jnp.bfloat16))
        x2 = x2 + _moe_ffn(_rmsnorm(x2, norm_ffn_g[l]), router_w, router_bias,
                           w_e_gate[l], w_e_up[l], w_e_down[l])
    return _rmsnorm(x2, norm_final_g).reshape(B, S, D)
```
